```python
import jax
import jax.numpy as jnp
from jax import lax
import numpy as np

D_MODEL = 2048
BATCH = 2
SEQ = 4096
DEPTH = 2
DEC_BATCH = 128
DEC_SEQ = 1
PAST_LEN = 2048
PAGE_SIZE = 128

A_PAIRS = ((128, 1), (512, 4), (2048, 16))
A_GROUPS = 3
A_HEADS = 4
A_HD = 128
A_WIDTH = A_HEADS * A_HD
N_BUCKETS = 32
BUCKET_MAX_DIST = 2048
Q_BLOCK = 128
B_HEADS = 4
B_DK = 128
B_DV = 256
B_QK = B_HEADS * B_DK
B_WIDTH = B_HEADS * B_DV
RET_CHUNK = 128
ROPE_BASE = 10000.0
C_WIDTH = 512
C_BLOCKS = 4
C_BD = C_WIDTH // C_BLOCKS
CONV_W = 4
LRU_C = 8.0
MIX_WIDTH = A_WIDTH + B_WIDTH + C_WIDTH
IN_WIDTH = 3 * A_GROUPS * A_WIDTH + A_WIDTH + 2 * B_QK + 2 * B_WIDTH + 2 * C_WIDTH
N_MEM = 256
X_HEADS = 4
X_HD = 128
X_WIDTH = X_HEADS * X_HD
EPS = 1e-6

kernel_name = 'hymba_dilated_retnet_rglru_step'


def _in_splits():
    sizes = (A_GROUPS * A_WIDTH,) * 3 + (A_WIDTH, B_QK, B_QK, B_WIDTH, B_WIDTH, C_WIDTH, C_WIDTH)
    return [int(s) for s in np.cumsum(sizes)[:-1]]


def rmsnorm(x, g):
    xf = x.astype(jnp.float32)
    y = xf * lax.rsqrt(jnp.mean(xf * xf, axis=-1, keepdims=True) + EPS)
    return (y * g.astype(jnp.float32)).astype(x.dtype)


def head_rmsnorm(x):
    xf = x.astype(jnp.float32)
    return xf * lax.rsqrt(jnp.mean(xf * xf, axis=-1, keepdims=True) + EPS)


def t5_bucket(dist):
    max_exact = N_BUCKETS // 2
    d = np.maximum(dist, 1).astype(np.float32)
    large = max_exact + (np.log(d / max_exact) / np.log(BUCKET_MAX_DIST / max_exact)
                         * (N_BUCKETS - max_exact)).astype(np.int32)
    large = np.minimum(large, N_BUCKETS - 1)
    return np.where(dist < max_exact, dist, large).astype(np.int32)


def a_bias_tables(rel_bias):
    tabs = []
    for g, (win, dil) in enumerate(A_PAIRS):
        dist = dil * np.arange(win // dil + 1)
        tabs.append(rel_bias[t5_bucket(dist), g])
    return tabs


def dilated_group(q, k, v, q_idx, dil, n_keys, bias):
    j = jnp.arange(n_keys)
    idx = q_idx[:, None] - dil * j[None, :]
    valid = idx >= 0
    idx = jnp.maximum(idx, 0)
    kg = jnp.take(k, idx, axis=1)
    vg = jnp.take(v, idx, axis=1)
    s = jnp.einsum('bthd,btjhd->btjh', q, kg).astype(jnp.float32) + bias.astype(jnp.float32)[None, None]
    s = jnp.where(valid[None, :, :, None], s, -jnp.inf)
    m = jnp.max(s, axis=2)
    p = jnp.exp(s - m[:, :, None, :])
    l = jnp.sum(p, axis=2)
    acc = jnp.einsum('btjh,btjhd->bthd', p.astype(v.dtype), vg).astype(jnp.float32)
    return acc, m, l


def combine_groups(parts, dtype):
    m = jnp.max(jnp.stack([p[1] for p in parts]), axis=0)
    num = 0.0
    den = 0.0
    for acc, mg, lg in parts:
        w = jnp.exp(mg - m)
        num = num + w[..., None] * acc
        den = den + w * lg
    return (num / den[..., None]).astype(dtype)


def dilated_prompt(q, k, v, bias_tabs):
    Bsz, S = q.shape[0], q.shape[1]
    nblk = S // Q_BLOCK
    qb = q.reshape(Bsz, nblk, Q_BLOCK, A_GROUPS, A_HEADS, A_HD).swapaxes(0, 1)
    starts = jnp.arange(nblk) * Q_BLOCK

    def body(args):
        q_blk, t0 = args
        q_idx = t0 + jnp.arange(Q_BLOCK)
        parts = [dilated_group(q_blk[:, :, g], k[:, :, g], v[:, :, g], q_idx, dil, win // dil + 1, bias_tabs[g])
                 for g, (win, dil) in enumerate(A_PAIRS)]
        return combine_groups(parts, q.dtype)

    out = lax.map(body, (qb, starts))
    return out.swapaxes(0, 1).reshape(Bsz, S, A_HEADS, A_HD)


def dilated_sample(q, k, v, bufs, bias_tabs):
    T = q.shape[1]
    parts = []
    for g, ((win, dil), buf) in enumerate(zip(A_PAIRS, bufs)):
        L = buf.shape[1]
        kk = jnp.concatenate([buf[:, :, 0].astype(k.dtype), k[:, :, g]], axis=1)
        vv = jnp.concatenate([buf[:, :, 1].astype(v.dtype), v[:, :, g]], axis=1)
        parts.append(dilated_group(q[:, :, g], kk, vv, L + jnp.arange(T), dil, win // dil + 1, bias_tabs[g]))
    return combine_groups(parts, q.dtype)


def rotary(x, pos):
    half = x.shape[-1] // 2
    inv = 1.0 / (ROPE_BASE ** jnp.linspace(0.0, 1.0, half, dtype=jnp.float32))
    ang = pos.astype(jnp.float32)[:, None] * inv[None, :]
    cos = jnp.cos(ang)[None, :, None, :]
    sin = jnp.sin(ang)[None, :, None, :]
    xf = x.astype(jnp.float32)
    x1, x2 = xf[..., :half], xf[..., half:]
    return jnp.concatenate([x1 * cos - x2 * sin, x1 * sin + x2 * cos], axis=-1).astype(x.dtype)


def ret_log_decay():
    return jnp.log1p(-jnp.exp2(-5.0 - jnp.arange(B_HEADS, dtype=jnp.float32)))


def retention_chunk(q, k, v, S0):
    T = q.shape[1]
    log_g = ret_log_decay()
    i = jnp.arange(T, dtype=jnp.float32)
    diff = i[:, None] - i[None, :]
    decay = jnp.where(diff >= 0, jnp.exp(jnp.maximum(diff, 0.0)[None] * log_g[:, None, None]), 0.0)
    qf, kf, vf = q.astype(jnp.float32), k.astype(jnp.float32), v.astype(jnp.float32)
    S0 = S0.astype(jnp.float32)
    s = jnp.einsum('bthd,bshd->bhts', qf, kf) * decay[None]
    inner = jnp.einsum('bhts,bshe->bthe', s, vf)
    q_dec = jnp.exp((i + 1.0)[:, None] * log_g[None, :])
    cross = jnp.einsum('bthd,bhde->bthe', qf, S0) * q_dec[None, :, :, None]
    k_dec = jnp.exp((T - 1.0 - i)[:, None] * log_g[None, :])
    S1 = (jnp.exp(T * log_g)[None, :, None, None] * S0
          + jnp.einsum('bthd,bthe->bhde', kf * k_dec[None, :, :, None], vf))
    return inner + cross, S1


def retention(q, k, v, S0):
    T = q.shape[1]
    if T <= RET_CHUNK:
        return retention_chunk(q, k, v, S0)
    nc = T // RET_CHUNK

    def split(a):
        return a.reshape(a.shape[0], nc, RET_CHUNK, *a.shape[2:]).swapaxes(0, 1)

    def step(S, xs):
        o, S_new = retention_chunk(xs[0], xs[1], xs[2], S)
        return S_new, o

    S1, o = lax.scan(step, S0.astype(jnp.float32), (split(q), split(k), split(v)))
    o = o.swapaxes(0, 1).reshape(q.shape[0], T, B_HEADS, B_DV)
    return o, S1


def _lru_combine(left, right):
    return left[0] * right[0], right[0] * left[1] + right[1]


def conv_rglru(xc, conv_buf, h0, conv_w, conv_b, wa, ba, wx, bx, lam):
    Bsz, T, C = xc.shape
    xp = jnp.concatenate([conv_buf.astype(xc.dtype), xc], axis=1)
    u = conv_b
    for tap in range(CONV_W):
        u = u + xp[:, tap:tap + T] * conv_w[tap]
    ub = u.reshape(Bsz, T, C_BLOCKS, C_BD)
    r = jax.nn.sigmoid((jnp.einsum('btnc,ncd->btnd', ub, wa).reshape(Bsz, T, C) + ba).astype(jnp.float32))
    ig = jax.nn.sigmoid((jnp.einsum('btnc,ncd->btnd', ub, wx).reshape(Bsz, T, C) + bx).astype(jnp.float32))
    log_a = -LRU_C * r * jax.nn.softplus(-lam.astype(jnp.float32))
    a = jnp.exp(log_a)
    b = jnp.sqrt(-jnp.expm1(2.0 * log_a)) * ig * u.astype(jnp.float32)
    b = b.at[:, 0].add(a[:, 0] * h0.astype(jnp.float32))
    _, h = lax.associative_scan(_lru_combine, (a, b), axis=1)
    return h.astype(xc.dtype), xp[:, -(CONV_W - 1):], h[:, -1]


def mix_sublayer(x, pos, a_bufs, S0, conv_buf, h0, bias_tabs,
                 norm_g, w_in, conv_w, conv_b, wa, ba, wx, bx, lam, w_out):
    Bsz, T, _ = x.shape
    xn = rmsnorm(x, norm_g)
    aq, ak, av, ag, bq, bk, bv, bg, cx, cg = jnp.split(xn @ w_in, _in_splits(), axis=-1)
    shp = (Bsz, T, A_GROUPS, A_HEADS, A_HD)
    aq = aq.reshape(shp) * (A_HD ** -0.5)
    ak = ak.reshape(shp)
    av = av.reshape(shp)
    if a_bufs is None:
        a_o = dilated_prompt(aq, ak, av, bias_tabs)
        a_new = [jnp.stack([ak[:, T - min(win, T):, g], av[:, T - min(win, T):, g]], axis=2)
                 for g, (win, _) in enumerate(A_PAIRS)]
    else:
        a_o = dilated_sample(aq, ak, av, a_bufs, bias_tabs)
        a_new = [jnp.stack([ak[:, :, g], av[:, :, g]], axis=2) for g in range(A_GROUPS)]
    a_o = a_o.reshape(Bsz, T, A_WIDTH) * jax.nn.silu(ag)
    bq = rotary(bq.reshape(Bsz, T, B_HEADS, B_DK), pos)
    bk = rotary(bk.reshape(Bsz, T, B_HEADS, B_DK), pos) * (B_DK ** -0.5)
    bv = bv.reshape(Bsz, T, B_HEADS, B_DV)
    b_o, S1 = retention(bq, bk, bv, S0)
    b_o = head_rmsnorm(b_o).reshape(Bsz, T, B_WIDTH).astype(x.dtype) * jax.nn.silu(bg)
    c_o, conv_new, h_last = conv_rglru(cx, conv_buf, h0, conv_w, conv_b, wa, ba, wx, bx, lam)
    c_o = c_o * jax.nn.silu(cg)
    y = jnp.concatenate([a_o, b_o, c_o], axis=-1) @ w_out
    return x + y, a_new, S1, conv_new, h_last


def mem_kv_proj(mem, norm_g, w_ckv):
    Bsz, N, _ = mem.shape
    return (rmsnorm(mem, norm_g) @ w_ckv).reshape(Bsz, N, 2, X_HEADS, X_HD)


def cross_sublayer(x, mem_kv, norm_g, w_cq, w_co):
    Bsz, T, _ = x.shape
    q = (rmsnorm(x, norm_g) @ w_cq).reshape(Bsz, T, X_HEADS, X_HD) * (X_HD ** -0.5)
    mk = mem_kv[:, :, 0].astype(x.dtype)
    mv = mem_kv[:, :, 1].astype(x.dtype)
    s = jnp.einsum('bthd,bnhd->bhtn', q, mk).astype(jnp.float32)
    p = jax.nn.softmax(s, axis=-1).astype(x.dtype)
    o = jnp.einsum('bhtn,bnhd->bthd', p, mv).reshape(Bsz, T, X_WIDTH)
    return x + o @ w_co


def setup_inputs(seed: int = 0) -> dict:
    key = jax.random.key(seed)
    ks = jax.random.split(key, 27)
    f32 = jnp.float32

    def nrm(i, shape, scale):
        return jax.random.normal(ks[i], shape, f32) * scale

    L1, L2, L3 = (min(win, PAST_LEN) for win, _ in A_PAIRS)
    a_init = jax.random.uniform(ks[19], (DEPTH, C_WIDTH), f32, 0.9, 0.999)
    return {
        'x_prompt': nrm(0, (BATCH, SEQ, D_MODEL), 1.0),
        'x_sample': nrm(1, (DEC_BATCH, DEC_SEQ, D_MODEL), 1.0),
        'cache_a1_kv': nrm(2, (DEPTH, DEC_BATCH, L1, 2, A_HEADS, A_HD), 1.0),
        'cache_a2_kv': nrm(3, (DEPTH, DEC_BATCH, L2, 2, A_HEADS, A_HD), 1.0),
        'cache_a3_kv': nrm(4, (DEPTH, DEC_BATCH, L3, 2, A_HEADS, A_HD), 1.0),
        'state_ret': nrm(5, (DEPTH, DEC_BATCH, B_HEADS, B_DK, B_DV), 0.5),
        'state_conv': nrm(6, (DEPTH, DEC_BATCH, CONV_W - 1, C_WIDTH), 1.0),
        'state_lru': nrm(7, (DEPTH, DEC_BATCH, C_WIDTH), 0.5),
        'cache_mem_kv': nrm(8, (DEPTH, DEC_BATCH, N_MEM, 2, X_HEADS, X_HD), 1.0),
        'mem_prompt': nrm(9, (BATCH, N_MEM, D_MODEL), 1.0),
        'a_rel_bias': nrm(10, (N_BUCKETS, A_GROUPS, A_HEADS), 0.2),
        'norm_mix': 1.0 + nrm(11, (DEPTH, D_MODEL), 0.02),
        'w_in': nrm(12, (DEPTH, D_MODEL, IN_WIDTH), D_MODEL ** -0.5),
        'conv_w': nrm(13, (DEPTH, CONV_W, C_WIDTH), CONV_W ** -0.5),
        'conv_b': nrm(14, (DEPTH, C_WIDTH), 0.02),
        'lru_wa': nrm(15, (DEPTH, C_BLOCKS, C_BD, C_BD), C_BD ** -0.5),
        'lru_ba': nrm(16, (DEPTH, C_WIDTH), 0.02),
        'lru_wx': nrm(17, (DEPTH, C_BLOCKS, C_BD, C_BD), C_BD ** -0.5),
        'lru_bx': nrm(18, (DEPTH, C_WIDTH), 0.02),
        'lru_lambda': jnp.log(a_init) - jnp.log1p(-a_init),
        'w_out': nrm(20, (DEPTH, MIX_WIDTH, D_MODEL), MIX_WIDTH ** -0.5),
        'norm_cross': 1.0 + nrm(21, (DEPTH, D_MODEL), 0.02),
        'norm_mem': 1.0 + nrm(22, (DEPTH, D_MODEL), 0.02),
        'w_cq': nrm(23, (DEPTH, D_MODEL, X_WIDTH), D_MODEL ** -0.5),
        'w_ckv': nrm(24, (DEPTH, D_MODEL, 2 * X_WIDTH), D_MODEL ** -0.5),
        'w_co': nrm(25, (DEPTH, X_WIDTH, D_MODEL), X_WIDTH ** -0.5),
        'norm_final': 1.0 + nrm(26, (D_MODEL,), 0.02),
    }


def reference(x_prompt, x_sample, cache_a1_kv, cache_a2_kv, cache_a3_kv, state_ret, state_conv, state_lru,
              cache_mem_kv, mem_prompt, a_rel_bias, norm_mix, w_in, conv_w, conv_b, lru_wa, lru_ba, lru_wx,
              lru_bx, lru_lambda, w_out, norm_cross, norm_mem, w_cq, w_ckv, w_co, norm_final):
    bias_tabs = a_bias_tables(a_rel_bias)
    Bp, Sp, _ = x_prompt.shape
    Ts = x_sample.shape[1]
    pos_p = jnp.arange(Sp)
    pos_s = PAST_LEN + jnp.arange(Ts)
    S0_p = jnp.zeros((Bp, B_HEADS, B_DK, B_DV), jnp.float32)
    conv0_p = jnp.zeros((Bp, CONV_W - 1, C_WIDTH), x_prompt.dtype)
    h0_p = jnp.zeros((Bp, C_WIDTH), jnp.float32)
    hp, hs = x_prompt, x_sample
    a_p = [[], [], []]
    a_s = [[], [], []]
    ret_p, ret_s, conv_p, conv_s, lru_p, lru_s, memkv_p = [], [], [], [], [], [], []
    for l in range(DEPTH):
        lw = (norm_mix[l], w_in[l], conv_w[l], conv_b[l], lru_wa[l], lru_ba[l], lru_wx[l], lru_bx[l],
              lru_lambda[l], w_out[l])
        hp, a_new, S1, cb, hl = mix_sublayer(hp, pos_p, None, S0_p, conv0_p, h0_p, bias_tabs, *lw)
        mkv = mem_kv_proj(mem_prompt, norm_mem[l], w_ckv[l])
        hp = cross_sublayer(hp, mkv, norm_cross[l], w_cq[l], w_co[l])
        for g in range(A_GROUPS):
            a_p[g].append(a_new[g])
        ret_p.append(S1)
        conv_p.append(cb)
        lru_p.append(hl)
        memkv_p.append(mkv)
        bufs = (cache_a1_kv[l], cache_a2_kv[l], cache_a3_kv[l])
        hs, a_new, S1, cb, hl = mix_sublayer(hs, pos_s, bufs, state_ret[l], state_conv[l], state_lru[l],
                                             bias_tabs, *lw)
        hs = cross_sublayer(hs, cache_mem_kv[l], norm_cross[l], w_cq[l], w_co[l])
        for g in range(A_GROUPS):
            a_s[g].append(a_new[g])
        ret_s.append(S1)
        conv_s.append(cb)
        lru_s.append(hl)
    y_prompt = rmsnorm(hp, norm_final)
    y_sample = rmsnorm(hs, norm_final)
    a1_prompt, a2_prompt, a3_prompt = (jnp.stack(a_p[g]) for g in range(A_GROUPS))
    a1_sample, a2_sample, a3_sample = (jnp.stack(a_s[g]) for g in range(A_GROUPS))
    ret_prompt, ret_sample = jnp.stack(ret_p), jnp.stack(ret_s)
    conv_prompt, conv_sample = jnp.stack(conv_p), jnp.stack(conv_s)
    lru_prompt, lru_sample = jnp.stack(lru_p), jnp.stack(lru_s)
    mem_kv_prompt = jnp.stack(memkv_p)
    return (y_prompt, y_sample, a1_prompt, a1_sample, a2_prompt, a2_sample, a3_prompt, a3_sample,
            ret_prompt, ret_sample, conv_prompt, conv_sample, lru_prompt, lru_sample, mem_kv_prompt)
```

```python
import functools

import numpy as np
import jax
import jax.numpy as jnp
from jax import lax
from jax.experimental import pallas as pl
from jax.experimental.pallas import tpu as pltpu

F32 = jnp.float32
BF16 = jnp.bfloat16

D_MODEL = 2048
PAST_LEN = 2048
A_PAIRS = ((128, 1), (512, 4), (2048, 16))
A_GROUPS = 3
A_HEADS = 4
A_HD = 128
A_WIDTH = A_HEADS * A_HD
A_KEYS = 128
N_BUCKETS = 32
BUCKET_MAX_DIST = 2048
B_HEADS = 4
B_DK = 128
B_DV = 256
B_QK = B_HEADS * B_DK
B_WIDTH = B_HEADS * B_DV
RET_CHUNK = 128
ROPE_BASE = 10000.0
C_WIDTH = 512
C_BLOCKS = 4
C_BD = C_WIDTH // C_BLOCKS
CONV_W = 4
LRU_C = 8.0
MIX_WIDTH = A_WIDTH + B_WIDTH + C_WIDTH
IN_WIDTH = 3 * A_GROUPS * A_WIDTH + A_WIDTH + 2 * B_QK + 2 * B_WIDTH + 2 * C_WIDTH
N_MEM = 256
X_HEADS = 4
X_HD = 128
X_WIDTH = X_HEADS * X_HD
EPS = 1e-6

OFF_AQ = 0
OFF_AK = A_GROUPS * A_WIDTH
OFF_AV = 2 * A_GROUPS * A_WIDTH
OFF_AG = 3 * A_GROUPS * A_WIDTH
OFF_BQ = OFF_AG + A_WIDTH
OFF_BK = OFF_BQ + B_QK
OFF_BV = OFF_BK + B_QK
OFF_BG = OFF_BV + B_WIDTH
OFF_CX = OFF_BG + B_WIDTH
OFF_CG = OFF_CX + C_WIDTH

V7X_VMEM_BYTES = 64 * 1024 * 1024
V7X_SUBLANES = 8
V7X_LANES = 128
MASK_VALUE = -1e30

Q_BLOCK = 128
LRU_CHUNK = 512
MM_TILE = 1024
FUSE_TILE = 256
DEC_BLOCK = 8


def _params(n_grid, vmem_bytes):
    limit = int(min(max(vmem_bytes, 16 * 1024 * 1024), V7X_VMEM_BYTES - 6 * 1024 * 1024))
    return pltpu.CompilerParams(dimension_semantics=("arbitrary",) * n_grid, vmem_limit_bytes=limit)


def _nbytes(shape, dtype):
    return int(np.prod(shape)) * jnp.dtype(dtype).itemsize


def _dot(a, b):
    return jnp.dot(a, b, preferred_element_type=F32)


def _dot_nt(a, b):
    return lax.dot_general(a, b, (((1,), (1,)), ((), ())), preferred_element_type=F32)


def _silu(x):
    return x * jax.nn.sigmoid(x)


def _rms(x, g):
    y = x * lax.rsqrt(jnp.mean(x * x, axis=-1, keepdims=True) + EPS)
    return y * g


def _rmsnorm_kernel(x_ref, g_ref, o_ref):
    o_ref[...] = _rms(x_ref[...].astype(F32), g_ref[...]).astype(o_ref.dtype)


def rmsnorm_rows(x, g, out_dtype, tm):
    m, d = x.shape
    tm = min(tm, m)
    return pl.pallas_call(
        _rmsnorm_kernel,
        out_shape=jax.ShapeDtypeStruct((m, d), out_dtype),
        grid=(m // tm,),
        in_specs=[pl.BlockSpec((tm, d), lambda i: (i, 0)), pl.BlockSpec((1, d), lambda i: (0, 0))],
        out_specs=pl.BlockSpec((tm, d), lambda i: (i, 0)),
        compiler_params=_params(1, 4 * _nbytes((tm, d), F32) + 2 * _nbytes((tm, d), out_dtype) * 2),
        name="rmsnorm_rows",
    )(x, g.reshape(1, d))


def _matmul_kernel(x_ref, w_ref, o_ref):
    o_ref[...] = _dot(x_ref[...], w_ref[...])


def _matmul_res_kernel(x_ref, w_ref, r_ref, o_ref):
    o_ref[...] = r_ref[...] + _dot(x_ref[...], w_ref[...])


def matmul(x, w, res=None, tm=MM_TILE, tn=MM_TILE):
    m, k = x.shape
    n = w.shape[1]
    tm, tn = min(tm, m), min(tn, n)
    in_specs = [pl.BlockSpec((tm, k), lambda j, i: (i, 0)), pl.BlockSpec((k, tn), lambda j, i: (0, j))]
    args = [x, w]
    vm = 2 * (_nbytes((tm, k), BF16) + _nbytes((k, tn), BF16) + 2 * _nbytes((tm, tn), F32))
    if res is not None:
        in_specs.append(pl.BlockSpec((tm, tn), lambda j, i: (i, j)))
        args.append(res)
        vm += 2 * _nbytes((tm, tn), F32)
    return pl.pallas_call(
        _matmul_kernel if res is None else _matmul_res_kernel,
        out_shape=jax.ShapeDtypeStruct((m, n), F32),
        grid=(n // tn, m // tm),
        in_specs=in_specs,
        out_specs=pl.BlockSpec((tm, tn), lambda j, i: (i, j)),
        compiler_params=_params(2, vm + 4 * 1024 * 1024),
        name="proj_matmul",
    )(*args)


def _attn_group_kernel(first, last, *refs):
    q_ref, kc_ref, kp_ref, vc_ref, vp_ref, bias_ref = refs[:6]
    pos = 6
    if not first:
        acc_in_ref, st_in_ref = refs[pos:pos + 2]
        pos += 2
    if last:
        ag_ref = refs[pos]
        out_ref = refs[pos + 1]
    else:
        acc_out_ref, st_out_ref = refs[pos:pos + 2]

    i = pl.program_id(2)
    col = lax.broadcasted_iota(jnp.int32, (Q_BLOCK, 2 * Q_BLOCK), 1)
    no_prev = jnp.logical_and(i == 0, col < Q_BLOCK)
    lane = lax.broadcasted_iota(jnp.int32, (Q_BLOCK, V7X_LANES), 1)
    stats = jnp.zeros((Q_BLOCK, V7X_LANES), F32)
    for h in range(A_HEADS):
        hs = slice(h * A_HD, (h + 1) * A_HD)
        q = (q_ref[:, hs] * (A_HD ** -0.5)).astype(BF16)
        k = jnp.concatenate([kp_ref[:, hs], kc_ref[:, hs]], axis=0).astype(BF16)
        v = jnp.concatenate([vp_ref[:, hs], vc_ref[:, hs]], axis=0).astype(BF16)
        s = _dot_nt(q, k) + bias_ref[h]
        s = jnp.where(no_prev, MASK_VALUE, s)
        m = jnp.max(s, axis=-1, keepdims=True)
        p = jnp.exp(s - m)
        l = jnp.sum(p, axis=-1, keepdims=True)
        acc = _dot(p.astype(BF16), v)
        if not first:
            m_in = st_in_ref[:, h:h + 1]
            l_in = st_in_ref[:, A_HEADS + h:A_HEADS + h + 1]
            m_new = jnp.maximum(m_in, m)
            w_in = jnp.exp(m_in - m_new)
            w_g = jnp.exp(m - m_new)
            acc = w_in * acc_in_ref[:, hs] + w_g * acc
            l = w_in * l_in + w_g * l
            m = m_new
        if last:
            out_ref[:, hs] = ((acc / l) * _silu(ag_ref[:, hs])).astype(out_ref.dtype)
        else:
            acc_out_ref[:, hs] = acc
            stats = jnp.where(lane == h, m, stats)
            stats = jnp.where(lane == A_HEADS + h, l, stats)
    if not last:
        st_out_ref[...] = stats


def attn_group(proj, g, bias, carry, bsz, seq):
    dil = A_PAIRS[g][1]
    first, last = carry is None, g == A_GROUPS - 1
    sub = seq // dil
    nqb = sub // Q_BLOCK
    cb = IN_WIDTH // A_WIDTH
    pv = proj.reshape(bsz, sub, dil * IN_WIDTH)

    def col_spec(blk, prev=False):
        if prev:
            return pl.BlockSpec((None, Q_BLOCK, A_WIDTH),
                                lambda b, r, i: (b, jnp.maximum(i - 1, 0), r * cb + blk))
        return pl.BlockSpec((None, Q_BLOCK, A_WIDTH), lambda b, r, i: (b, i, r * cb + blk))

    def nat_spec(width):
        return pl.BlockSpec((None, Q_BLOCK, width), lambda b, r, i: (b, i, r))

    kblk, vblk = OFF_AK // A_WIDTH + g, OFF_AV // A_WIDTH + g
    in_specs = [col_spec(OFF_AQ // A_WIDTH + g), col_spec(kblk), col_spec(kblk, True),
                col_spec(vblk), col_spec(vblk, True),
                pl.BlockSpec((A_HEADS, Q_BLOCK, 2 * Q_BLOCK), lambda b, r, i: (0, 0, 0))]
    args = [pv, pv, pv, pv, pv, bias]
    if not first:
        acc_in, st_in = carry
        in_specs += [nat_spec(A_WIDTH), nat_spec(V7X_LANES)]
        args += [acc_in.reshape(bsz, sub, dil * A_WIDTH), st_in.reshape(bsz, sub, dil * V7X_LANES)]
    if last:
        in_specs.append(col_spec(OFF_AG // A_WIDTH))
        args.append(pv)
        out_shape = jax.ShapeDtypeStruct((bsz, sub, dil * A_WIDTH), BF16)
        out_specs = nat_spec(A_WIDTH)
    else:
        out_shape = (jax.ShapeDtypeStruct((bsz, sub, dil * A_WIDTH), F32),
                     jax.ShapeDtypeStruct((bsz, sub, dil * V7X_LANES), F32))
        out_specs = (nat_spec(A_WIDTH), nat_spec(V7X_LANES))
    blk = _nbytes((Q_BLOCK, A_WIDTH), F32)
    out = pl.pallas_call(
        functools.partial(_attn_group_kernel, first, last),
        out_shape=out_shape,
        grid=(bsz, dil, nqb),
        in_specs=in_specs,
        out_specs=out_specs,
        compiler_params=_params(3, 2 * 9 * blk + 2 * _nbytes(bias.shape, F32) + 8 * 1024 * 1024),
        name=f"attn_group{g}",
    )(*args)
    if last:
        return out.reshape(bsz * seq, A_WIDTH)
    acc, st = out
    return acc.reshape(bsz * seq, A_WIDTH), st.reshape(bsz * seq, V7X_LANES)


def _rotate(x, cos2, sin2):
    return x * cos2 + pltpu.roll(x, B_DK // 2, axis=1) * sin2


def _retention_kernel(nc, q_ref, k_ref, v_ref, g_ref, cos_ref, sin_ref, decay_ref, qdec_ref, kdec_ref,
                      gt_ref, o_ref, s_out_ref, s_scr):
    c = pl.program_id(1)

    @pl.when(c == 0)
    def _():
        s_scr[...] = jnp.zeros_like(s_scr)

    cos2, sin2 = cos_ref[...], sin_ref[...]
    for h in range(B_HEADS):
        hs = slice(h * B_DK, (h + 1) * B_DK)
        vs = slice(h * B_DV, (h + 1) * B_DV)
        q = _rotate(q_ref[:, hs], cos2, sin2)
        k = _rotate(k_ref[:, hs], cos2, sin2) * (B_DK ** -0.5)
        v = v_ref[:, vs].astype(BF16)
        s0 = s_scr[h]
        qb = q.astype(BF16)
        s = _dot_nt(qb, k.astype(BF16)) * decay_ref[h]
        inner = _dot(s.astype(BF16), v)
        cross = _dot(qb, s0.astype(BF16)) * qdec_ref[h]
        kd_t = jnp.transpose(k * kdec_ref[h]).astype(BF16)
        s_scr[h] = gt_ref[h] * s0 + _dot(kd_t, v)
        o = inner + cross
        o = o * lax.rsqrt(jnp.mean(o * o, axis=-1, keepdims=True) + EPS)
        o_ref[:, vs] = (o * _silu(g_ref[:, vs])).astype(o_ref.dtype)

    @pl.when(c == nc - 1)
    def _():
        s_out_ref[...] = s_scr[...]


def _ret_tables(t):
    log_g = jnp.log1p(-jnp.exp2(-5.0 - jnp.arange(B_HEADS, dtype=F32)))
    i = jnp.arange(t, dtype=F32)
    diff = i[:, None] - i[None, :]
    decay = jnp.where(diff >= 0, jnp.exp(jnp.maximum(diff, 0.0)[None] * log_g[:, None, None]), 0.0)
    q_dec = jnp.exp((i + 1.0)[None, :] * log_g[:, None])
    k_dec = jnp.exp((t - 1.0 - i)[None, :] * log_g[:, None])
    g_t = jnp.exp(t * log_g)
    return decay, q_dec, k_dec, g_t


def _rope_tables(pos):
    half = B_DK // 2
    inv = 1.0 / (ROPE_BASE ** jnp.linspace(0.0, 1.0, half, dtype=F32))
    ang = pos.astype(F32)[:, None] * inv[None, :]
    cos, sin = jnp.cos(ang), jnp.sin(ang)
    return jnp.concatenate([cos, cos], axis=-1), jnp.concatenate([-sin, sin], axis=-1)


def retention_prompt(proj, bsz, seq):
    nc = seq // RET_CHUNK
    pv = proj.reshape(bsz, seq, IN_WIDTH)
    cos2, sin2 = _rope_tables(jnp.arange(seq))
    decay, q_dec, k_dec, g_t = _ret_tables(RET_CHUNK)
    qdec_b = jnp.broadcast_to(q_dec[:, :, None], (B_HEADS, RET_CHUNK, B_DV))
    kdec_b = jnp.broadcast_to(k_dec[:, :, None], (B_HEADS, RET_CHUNK, B_DK))
    gt_b = jnp.broadcast_to(g_t[:, None, None], (B_HEADS, 1, B_DV))

    def col(width, off):
        return pl.BlockSpec((None, RET_CHUNK, width), lambda b, c: (b, c, off // width))

    def const(shape):
        return pl.BlockSpec(shape, lambda b, c: (0,) * len(shape))

    tab = pl.BlockSpec((RET_CHUNK, B_DK), lambda b, c: (c, 0))
    o, s1 = pl.pallas_call(
        functools.partial(_retention_kernel, nc),
        out_shape=(jax.ShapeDtypeStruct((bsz, seq, B_WIDTH), BF16),
                   jax.ShapeDtypeStruct((bsz, B_HEADS, B_DK, B_DV), F32)),
        grid=(bsz, nc),
        in_specs=[col(B_QK, OFF_BQ), col(B_QK, OFF_BK), col(B_WIDTH, OFF_BV), col(B_WIDTH, OFF_BG),
                  tab, tab, const(decay.shape), const(qdec_b.shape), const(kdec_b.shape), const(gt_b.shape)],
        out_specs=(pl.BlockSpec((None, RET_CHUNK, B_WIDTH), lambda b, c: (b, c, 0)),
                   pl.BlockSpec((None, B_HEADS, B_DK, B_DV), lambda b, c: (b, 0, 0, 0))),
        scratch_shapes=[pltpu.VMEM((B_HEADS, B_DK, B_DV), F32)],
        compiler_params=_params(2, 24 * 1024 * 1024),
        name="retention_prompt",
    )(pv, pv, pv, pv, cos2, sin2, decay, qdec_b, kdec_b, gt_b)
    return o.reshape(bsz * seq, B_WIDTH), s1


def _softplus(x):
    return jnp.maximum(x, 0.0) + jnp.log1p(jnp.exp(-jnp.abs(x)))


def _lru_coeffs(u, wa_ref, ba_ref, wx_ref, bx_ref, lam_ref):
    rs, igs = [], []
    for n in range(C_BLOCKS):
        ns = slice(n * C_BD, (n + 1) * C_BD)
        un = u[:, ns].astype(BF16)
        rs.append(jax.nn.sigmoid(_dot(un, wa_ref[n]) + ba_ref[:, ns]))
        igs.append(jax.nn.sigmoid(_dot(un, wx_ref[n]) + bx_ref[:, ns]))
    r = jnp.concatenate(rs, axis=-1)
    ig = jnp.concatenate(igs, axis=-1)
    log_a = -LRU_C * r * _softplus(-lam_ref[...])
    a = jnp.exp(log_a)
    th = jnp.tanh(log_a)
    b = jnp.sqrt(-2.0 * th / (1.0 - th)) * ig * u
    return a, b


def _conv_lru_kernel(nc, cx_ref, cg_ref, cw_ref, cb_ref, wa_ref, ba_ref, wx_ref, bx_ref, lam_ref,
                     o_ref, conv_out_ref, h_out_ref, xp_scr, a_scr, b_scr, h_scr):
    c = pl.program_id(1)
    t = LRU_CHUNK
    pad = V7X_SUBLANES

    @pl.when(c == 0)
    def _():
        xp_scr[0:pad, :] = jnp.zeros((pad, C_WIDTH), F32)
        h_scr[...] = jnp.zeros_like(h_scr)

    xp_scr[pad:pad + t, :] = cx_ref[...]
    u = cb_ref[...]
    for tap in range(CONV_W):
        start = pad - (CONV_W - 1) + tap
        u = u + xp_scr[start:start + t, :] * cw_ref[tap:tap + 1, :]
    a, b = _lru_coeffs(u, wa_ref, ba_ref, wx_ref, bx_ref, lam_ref)
    a_scr[...] = a
    b_scr[...] = b

    row = lax.broadcasted_iota(jnp.int32, (V7X_SUBLANES, C_WIDTH), 0)

    def body(i, h):
        r0 = pl.multiple_of(i * V7X_SUBLANES, V7X_SUBLANES)
        a8 = a_scr[pl.ds(r0, V7X_SUBLANES), :]
        b8 = b_scr[pl.ds(r0, V7X_SUBLANES), :]
        for k in (1, 2, 4):
            a_sh = jnp.where(row >= k, pltpu.roll(a8, k, axis=0), 1.0)
            b_sh = jnp.where(row >= k, pltpu.roll(b8, k, axis=0), 0.0)
            b8 = a8 * b_sh + b8
            a8 = a8 * a_sh
        hrows = a8 * h + b8
        b_scr[pl.ds(r0, V7X_SUBLANES), :] = hrows
        return jnp.broadcast_to(hrows[V7X_SUBLANES - 1:V7X_SUBLANES, :], (V7X_SUBLANES, C_WIDTH))

    h_last = lax.fori_loop(0, t // V7X_SUBLANES, body, h_scr[...])
    h_scr[...] = h_last
    o_ref[...] = (b_scr[...] * _silu(cg_ref[...])).astype(o_ref.dtype)
    tail = xp_scr[pad + t - (CONV_W - 1):pad + t, :]
    xp_scr[pad - (CONV_W - 1):pad, :] = tail

    @pl.when(c == nc - 1)
    def _():
        conv_out_ref[...] = tail
        h_out_ref[...] = h_last[0:1, :]


def conv_lru_prompt(proj, lw, bsz, seq):
    nc = seq // LRU_CHUNK
    pv = proj.reshape(bsz, seq, IN_WIDTH)

    def col(off):
        return pl.BlockSpec((None, LRU_CHUNK, C_WIDTH), lambda b, c: (b, c, off // C_WIDTH))

    def const(shape):
        return pl.BlockSpec(shape, lambda b, c: (0,) * len(shape))

    o, conv_new, h_last = pl.pallas_call(
        functools.partial(_conv_lru_kernel, nc),
        out_shape=(jax.ShapeDtypeStruct((bsz, seq, C_WIDTH), BF16),
                   jax.ShapeDtypeStruct((bsz, CONV_W - 1, C_WIDTH), F32),
                   jax.ShapeDtypeStruct((bsz, 1, C_WIDTH), F32)),
        grid=(bsz, nc),
        in_specs=[col(OFF_CX), col(OFF_CG), const((CONV_W, C_WIDTH)), const((1, C_WIDTH)),
                  const((C_BLOCKS, C_BD, C_BD)), const((1, C_WIDTH)),
                  const((C_BLOCKS, C_BD, C_BD)), const((1, C_WIDTH)), const((1, C_WIDTH))],
        out_specs=(pl.BlockSpec((None, LRU_CHUNK, C_WIDTH), lambda b, c: (b, c, 0)),
                   pl.BlockSpec((None, CONV_W - 1, C_WIDTH), lambda b, c: (b, 0, 0)),
                   pl.BlockSpec((None, 1, C_WIDTH), lambda b, c: (b, 0, 0))),
        scratch_shapes=[pltpu.VMEM((LRU_CHUNK + V7X_SUBLANES, C_WIDTH), F32),
                        pltpu.VMEM((LRU_CHUNK, C_WIDTH), F32),
                        pltpu.VMEM((LRU_CHUNK, C_WIDTH), F32),
                        pltpu.VMEM((V7X_SUBLANES, C_WIDTH), F32)],
        compiler_params=_params(2, 24 * 1024 * 1024),
        name="conv_lru_prompt",
    )(pv, pv, lw["conv_w"], lw["conv_b"], lw["wa"], lw["ba"], lw["wx"], lw["bx"], lw["lam"])
    return o.reshape(bsz * seq, C_WIDTH), conv_new, h_last.reshape(bsz, C_WIDTH)


def _fused_out_kernel(x_ref, a_ref, b_ref, c_ref, wo_ref, gc_ref, wq_ref, mkv_ref, wco_ref, gn_ref,
                      h_ref, n_ref):
    y = x_ref[...]
    y = y + _dot(a_ref[...], wo_ref[0:A_WIDTH, :])
    y = y + _dot(b_ref[...], wo_ref[A_WIDTH:A_WIDTH + B_WIDTH, :])
    y = y + _dot(c_ref[...], wo_ref[A_WIDTH + B_WIDTH:MIX_WIDTH, :])
    xc = _rms(y, gc_ref[...]).astype(BF16)
    q = _dot(xc, wq_ref[...]) * (X_HD ** -0.5)
    outs = []
    for h in range(X_HEADS):
        hs = slice(h * X_HD, (h + 1) * X_HD)
        mk = mkv_ref[:, hs].astype(BF16)
        mv = mkv_ref[:, X_WIDTH + h * X_HD:X_WIDTH + (h + 1) * X_HD].astype(BF16)
        s = _dot_nt(q[:, hs].astype(BF16), mk)
        e = jnp.exp(s - jnp.max(s, axis=-1, keepdims=True))
        p = e / jnp.sum(e, axis=-1, keepdims=True)
        outs.append(_dot(p.astype(BF16), mv))
    o = jnp.concatenate(outs, axis=-1).astype(BF16)
    y = y + _dot(o, wco_ref[...])
    h_ref[...] = y
    n_ref[...] = _rms(y, gn_ref[...]).astype(n_ref.dtype)


def fused_out(x, a_o, b_o, c_o, w_out, g_cross, w_cq, mkv, w_co, g_next, next_dtype, bsz, seq):
    m = x.shape[0]
    tm = FUSE_TILE
    tiles_per_seq = seq // tm

    def rows(width):
        return pl.BlockSpec((tm, width), lambda i: (i, 0))

    def const(shape):
        return pl.BlockSpec(shape, lambda i: (0,) * len(shape))

    vm = 2 * (_nbytes((MIX_WIDTH, D_MODEL), BF16) + 2 * _nbytes((D_MODEL, X_WIDTH), BF16)
              + 3 * _nbytes((tm, D_MODEL), F32) + _nbytes((tm, MIX_WIDTH), BF16)
              + _nbytes((N_MEM, 2 * X_WIDTH), F32)) + 12 * 1024 * 1024
    return pl.pallas_call(
        _fused_out_kernel,
        out_shape=(jax.ShapeDtypeStruct((m, D_MODEL), F32), jax.ShapeDtypeStruct((m, D_MODEL), next_dtype)),
        grid=(m // tm,),
        in_specs=[rows(D_MODEL), rows(A_WIDTH), rows(B_WIDTH), rows(C_WIDTH),
                  const((MIX_WIDTH, D_MODEL)), const((1, D_MODEL)), const((D_MODEL, X_WIDTH)),
                  pl.BlockSpec((None, N_MEM, 2 * X_WIDTH), lambda i: (i // tiles_per_seq, 0, 0)),
                  const((X_WIDTH, D_MODEL)), const((1, D_MODEL))],
        out_specs=(rows(D_MODEL), rows(D_MODEL)),
        compiler_params=_params(1, vm),
        name="fused_out_cross",
    )(x, a_o, b_o, c_o, w_out, g_cross.reshape(1, D_MODEL), w_cq, mkv, w_co, g_next.reshape(1, D_MODEL))


def _head_masks(width):
    row = lax.broadcasted_iota(jnp.int32, (V7X_SUBLANES, width), 0)
    lane = lax.broadcasted_iota(jnp.int32, (V7X_SUBLANES, width), 1)
    return jnp.logical_and(lane >= row * A_HD, lane < (row + 1) * A_HD)


def _attend_rows(q_row, kv, n_keys, bias, own):
    qbd = jnp.where(own, jnp.broadcast_to(q_row, (V7X_SUBLANES, A_WIDTH)), 0.0).astype(BF16)
    k = kv[:, 0:A_WIDTH].astype(BF16)
    v = kv[:, A_WIDTH:2 * A_WIDTH].astype(BF16)
    s = _dot_nt(qbd, k)
    if bias is not None:
        s = s + bias
    m = jnp.max(s, axis=-1, keepdims=True)
    return qbd, s, m, v


def _dec_self_attn_kernel(pq_ref, c1_ref, c2_ref, c3_ref, brev_ref, b0_ref, o_ref):
    own = _head_masks(A_WIDTH)
    caches = (c1_ref, c2_ref, c3_ref)

    def body(j, carry):
        num = jnp.zeros((V7X_SUBLANES, A_WIDTH), F32)
        den = jnp.zeros((V7X_SUBLANES, 1), F32)
        m_run = jnp.full((V7X_SUBLANES, 1), MASK_VALUE, F32)
        for g in range(A_GROUPS):
            q_row = pq_ref[pl.ds(j, 1), OFF_AQ + g * A_WIDTH:OFF_AQ + (g + 1) * A_WIDTH] * (A_HD ** -0.5)
            k_new = pq_ref[pl.ds(j, 1), OFF_AK + g * A_WIDTH:OFF_AK + (g + 1) * A_WIDTH]
            v_new = pq_ref[pl.ds(j, 1), OFF_AV + g * A_WIDTH:OFF_AV + (g + 1) * A_WIDTH]
            qbd, s, m, v = _attend_rows(q_row, caches[g][j], A_KEYS, brev_ref[g], own)
            k_new_r = k_new.astype(BF16).astype(F32)
            v_new_r = v_new.astype(BF16).astype(F32)
            s_self = jnp.sum(qbd.astype(F32) * k_new_r, axis=-1, keepdims=True) + b0_ref[g][:, 0:1]
            m = jnp.maximum(m, s_self)
            p = jnp.exp(s - m)
            p_self = jnp.exp(s_self - m)
            l = jnp.sum(p, axis=-1, keepdims=True) + p_self
            acc = _dot(p.astype(BF16), v) + p_self.astype(BF16).astype(F32) * v_new_r
            m_new = jnp.maximum(m_run, m)
            w_old = jnp.exp(m_run - m_new)
            w_g = jnp.exp(m - m_new)
            num = w_old * num + w_g * acc
            den = w_old * den + w_g * l
            m_run = m_new
        out8 = jnp.where(own, num / den, 0.0)
        ag = pq_ref[pl.ds(j, 1), OFF_AG:OFF_AG + A_WIDTH]
        o_ref[pl.ds(j, 1), :] = (jnp.sum(out8, axis=0, keepdims=True) * _silu(ag)).astype(o_ref.dtype)
        return carry

    lax.fori_loop(0, DEC_BLOCK, body, 0)


def dec_self_attn(proj_s, caches, layer, brev, b0):
    nb = proj_s.shape[0]
    width = OFF_AG + A_WIDTH
    row_w = 2 * A_WIDTH

    def cache_spec(dil):
        return pl.BlockSpec((None, DEC_BLOCK, A_KEYS, row_w), lambda i: (layer, i, 0, 0))

    cviews = [c.reshape(c.shape[0], nb, A_KEYS, A_PAIRS[g][1] * row_w) for g, c in enumerate(caches)]
    blk = _nbytes((DEC_BLOCK, A_KEYS, row_w), F32)
    return pl.pallas_call(
        _dec_self_attn_kernel,
        out_shape=jax.ShapeDtypeStruct((nb, A_WIDTH), F32),
        grid=(nb // DEC_BLOCK,),
        in_specs=[pl.BlockSpec((DEC_BLOCK, width), lambda i: (i, 0)),
                  cache_spec(1), cache_spec(4), cache_spec(16),
                  pl.BlockSpec(brev.shape, lambda i: (0, 0, 0)),
                  pl.BlockSpec(b0.shape, lambda i: (0, 0, 0))],
        out_specs=pl.BlockSpec((DEC_BLOCK, A_WIDTH), lambda i: (i, 0)),
        compiler_params=_params(1, 2 * 3 * blk + 8 * 1024 * 1024),
        name="dec_self_attn",
    )(proj_s, *cviews, brev, b0)


def _dec_cross_attn_kernel(q_ref, mkv_ref, o_ref):
    own = _head_masks(X_WIDTH)

    def body(j, carry):
        q_row = q_ref[pl.ds(j, 1), :] * (X_HD ** -0.5)
        _, s, m, v = _attend_rows(q_row, mkv_ref[j], N_MEM, None, own)
        e = jnp.exp(s - m)
        p = e / jnp.sum(e, axis=-1, keepdims=True)
        out8 = jnp.where(own, _dot(p.astype(BF16), v), 0.0)
        o_ref[pl.ds(j, 1), :] = jnp.sum(out8, axis=0, keepdims=True).astype(o_ref.dtype)
        return carry

    lax.fori_loop(0, DEC_BLOCK, body, 0)


def dec_cross_attn(q, cache_mem_kv, layer):
    nb = q.shape[0]
    mv = cache_mem_kv.reshape(cache_mem_kv.shape[0], nb, N_MEM, 2 * X_WIDTH)
    blk = _nbytes((DEC_BLOCK, N_MEM, 2 * X_WIDTH), F32)
    return pl.pallas_call(
        _dec_cross_attn_kernel,
        out_shape=jax.ShapeDtypeStruct((nb, X_WIDTH), F32),
        grid=(nb // DEC_BLOCK,),
        in_specs=[pl.BlockSpec((DEC_BLOCK, X_WIDTH), lambda i: (i, 0)),
                  pl.BlockSpec((None, DEC_BLOCK, N_MEM, 2 * X_WIDTH), lambda i: (layer, i, 0, 0))],
        out_specs=pl.BlockSpec((DEC_BLOCK, X_WIDTH), lambda i: (i, 0)),
        compiler_params=_params(1, 2 * blk + 8 * 1024 * 1024),
        name="dec_cross_attn",
    )(q, mv)


def _dec_retention_kernel(nb, p_ref, s_ref, cos_ref, sin_ref, g1_ref, o_ref, s_out_ref,
                          q_scr, k_scr, kt_scr, v_scr):
    i = pl.program_id(0)

    @pl.when(i == 0)
    def _():
        cos2, sin2 = cos_ref[...], sin_ref[...]
        for h in range(B_HEADS):
            hs = slice(h * B_DK, (h + 1) * B_DK)
            q = _rotate(p_ref[:, OFF_BQ + h * B_DK:OFF_BQ + (h + 1) * B_DK], cos2, sin2)
            k = _rotate(p_ref[:, OFF_BK + h * B_DK:OFF_BK + (h + 1) * B_DK], cos2, sin2) * (B_DK ** -0.5)
            q_scr[:, hs] = q.astype(BF16).astype(F32)
            k_scr[:, hs] = k.astype(BF16).astype(F32)
            kt_scr[h] = jnp.transpose(k).astype(BF16)
        v_scr[...] = p_ref[:, OFF_BV:OFF_BV + B_WIDTH].astype(BF16).astype(F32)

    r0 = pl.multiple_of(i * DEC_BLOCK, DEC_BLOCK)
    row8 = lax.broadcasted_iota(jnp.int32, (DEC_BLOCK, B_DK), 0)
    rows = lax.broadcasted_iota(jnp.int32, (nb, B_DV), 0)
    for h in range(B_HEADS):
        hs = slice(h * B_DK, (h + 1) * B_DK)
        vs = slice(h * B_DV, (h + 1) * B_DV)
        q8 = q_scr[pl.ds(r0, DEC_BLOCK), hs]
        k8 = k_scr[pl.ds(r0, DEC_BLOCK), hs]
        v8 = v_scr[pl.ds(r0, DEC_BLOCK), vs]
        v_all = v_scr[:, vs]
        g1 = g1_ref[h]
        cross = jnp.zeros((DEC_BLOCK, B_DV), F32)
        for j in range(DEC_BLOCK):
            s0 = s_ref[j, h]
            qj = jnp.where(row8 == j, q8, 0.0).astype(BF16)
            cross = cross + _dot(qj, s0.astype(BF16))
            vm = jnp.where(rows == r0 + j, v_all, 0.0).astype(BF16)
            s_out_ref[j, h] = g1 * s0 + _dot(kt_scr[h], vm)
        s_qk = jnp.sum(q8 * k8, axis=-1, keepdims=True)
        o = s_qk.astype(BF16).astype(F32) * v8 + cross * g1
        o = o * lax.rsqrt(jnp.mean(o * o, axis=-1, keepdims=True) + EPS)
        gate = p_ref[pl.ds(r0, DEC_BLOCK), OFF_BG + h * B_DV:OFF_BG + (h + 1) * B_DV]
        o_ref[:, vs] = (o * _silu(gate)).astype(o_ref.dtype)


def dec_retention(proj_s, state_ret, layer):
    nb = proj_s.shape[0]
    cos2, sin2 = _rope_tables(PAST_LEN + jnp.arange(1))
    _, q_dec, _, _ = _ret_tables(1)
    g1 = jnp.broadcast_to(q_dec[:, :, None], (B_HEADS, 1, B_DV))
    sblk = (None, DEC_BLOCK, B_HEADS, B_DK, B_DV)
    o, s1 = pl.pallas_call(
        functools.partial(_dec_retention_kernel, nb),
        out_shape=(jax.ShapeDtypeStruct((nb, B_WIDTH), F32),
                   jax.ShapeDtypeStruct((nb, B_HEADS, B_DK, B_DV), F32)),
        grid=(nb // DEC_BLOCK,),
        in_specs=[pl.BlockSpec((nb, IN_WIDTH), lambda i: (0, 0)),
                  pl.BlockSpec(sblk, lambda i: (layer, i, 0, 0, 0)),
                  pl.BlockSpec((1, B_DK), lambda i: (0, 0)), pl.BlockSpec((1, B_DK), lambda i: (0, 0)),
                  pl.BlockSpec((B_HEADS, 1, B_DV), lambda i: (0, 0, 0))],
        out_specs=(pl.BlockSpec((DEC_BLOCK, B_WIDTH), lambda i: (i, 0)),
                   pl.BlockSpec(sblk[1:], lambda i: (i, 0, 0, 0))),
        scratch_shapes=[pltpu.VMEM((nb, B_QK), F32), pltpu.VMEM((nb, B_QK), F32),
                        pltpu.VMEM((B_HEADS, B_DK, nb), BF16), pltpu.VMEM((nb, B_WIDTH), F32)],
        compiler_params=_params(1, 2 * _nbytes((nb, IN_WIDTH), F32)
                                + 4 * _nbytes((DEC_BLOCK, B_HEADS, B_DK, B_DV), F32) + 12 * 1024 * 1024),
        name="dec_retention",
    )(proj_s, state_ret, cos2, sin2, g1)
    return o, s1


def _dec_conv_lru_kernel(p_ref, sc_ref, h0_ref, cw_ref, cb_ref, wa_ref, ba_ref, wx_ref, bx_ref, lam_ref,
                         o_ref, conv_out_ref, h_out_ref):
    cx = p_ref[:, OFF_CX:OFF_CX + C_WIDTH]
    u = cb_ref[...]
    for tap in range(CONV_W - 1):
        u = u + sc_ref[:, tap * C_WIDTH:(tap + 1) * C_WIDTH] * cw_ref[tap:tap + 1, :]
    u = u + cx * cw_ref[CONV_W - 1:CONV_W, :]
    a, b = _lru_coeffs(u, wa_ref, ba_ref, wx_ref, bx_ref, lam_ref)
    h = a * h0_ref[...] + b
    h_out_ref[...] = h
    conv_out_ref[:, 0:(CONV_W - 2) * C_WIDTH] = sc_ref[:, C_WIDTH:(CONV_W - 1) * C_WIDTH]
    conv_out_ref[:, (CONV_W - 2) * C_WIDTH:(CONV_W - 1) * C_WIDTH] = cx
    o_ref[...] = (h * _silu(p_ref[:, OFF_CG:OFF_CG + C_WIDTH])).astype(o_ref.dtype)


def dec_conv_lru(proj_s, state_conv, state_lru, lw, layer):
    nb = proj_s.shape[0]
    cw = (CONV_W - 1) * C_WIDTH
    sc = state_conv.reshape(state_conv.shape[0], nb, cw)

    def const(shape):
        return pl.BlockSpec(shape, lambda i: (0,) * len(shape))

    o, conv_new, h = pl.pallas_call(
        _dec_conv_lru_kernel,
        out_shape=(jax.ShapeDtypeStruct((nb, C_WIDTH), F32),
                   jax.ShapeDtypeStruct((nb, cw), F32),
                   jax.ShapeDtypeStruct((nb, C_WIDTH), F32)),
        grid=(1,),
        in_specs=[const((nb, IN_WIDTH)),
                  pl.BlockSpec((None, nb, cw), lambda i: (layer, 0, 0)),
                  pl.BlockSpec((None, nb, C_WIDTH), lambda i: (layer, 0, 0)),
                  const((CONV_W, C_WIDTH)), const((1, C_WIDTH)),
                  const((C_BLOCKS, C_BD, C_BD)), const((1, C_WIDTH)),
                  const((C_BLOCKS, C_BD, C_BD)), const((1, C_WIDTH)), const((1, C_WIDTH))],
        out_specs=(const((nb, C_WIDTH)), const((nb, cw)), const((nb, C_WIDTH))),
        compiler_params=_params(1, 2 * _nbytes((nb, IN_WIDTH), F32) + 8 * 1024 * 1024),
        name="dec_conv_lru",
    )(proj_s, sc, state_lru, lw["conv_w"], lw["conv_b"], lw["wa"], lw["ba"], lw["wx"], lw["bx"], lw["lam"])
    return o, conv_new.reshape(nb, CONV_W - 1, C_WIDTH), h


def _t5_bucket(dist):
    max_exact = N_BUCKETS // 2
    d = np.maximum(dist, 1).astype(np.float32)
    large = max_exact + (np.log(d / max_exact) / np.log(BUCKET_MAX_DIST / max_exact)
                         * (N_BUCKETS - max_exact)).astype(np.int32)
    large = np.minimum(large, N_BUCKETS - 1)
    return np.where(dist < max_exact, dist, large).astype(np.int32)


def _bias_tables(rel_bias):
    qi = np.arange(Q_BLOCK)[:, None]
    cc = np.arange(2 * Q_BLOCK)[None, :]
    j = qi + Q_BLOCK - cc
    valid = (j >= 0) & (j <= A_KEYS)
    jc = np.clip(j, 0, A_KEYS)
    prompt, rev, own = [], [], []
    for g, (_, dil) in enumerate(A_PAIRS):
        tab = rel_bias[_t5_bucket(dil * np.arange(A_KEYS + 1)), g]
        full = jnp.where(valid[:, :, None], tab[jc], MASK_VALUE)
        prompt.append(jnp.transpose(full, (2, 0, 1)))
        pad = jnp.zeros((V7X_SUBLANES - A_HEADS, A_KEYS), F32)
        rev.append(jnp.concatenate([jnp.transpose(tab[A_KEYS - np.arange(A_KEYS)]), pad], axis=0))
        own0 = jnp.concatenate([tab[0], jnp.zeros((V7X_SUBLANES - A_HEADS,), F32)])
        own.append(jnp.broadcast_to(own0[:, None], (V7X_SUBLANES, V7X_LANES)))
    return prompt, jnp.stack(rev), jnp.stack(own)


def kernel(x_prompt, x_sample, cache_a1_kv, cache_a2_kv, cache_a3_kv, state_ret, state_conv, state_lru,
           cache_mem_kv, mem_prompt, a_rel_bias, norm_mix, w_in, conv_w, conv_b, lru_wa, lru_ba, lru_wx,
           lru_bx, lru_lambda, w_out, norm_cross, norm_mem, w_cq, w_ckv, w_co, norm_final):
    bp, sp, _ = x_prompt.shape
    nb, ts, _ = x_sample.shape
    depth = w_in.shape[0]
    assert ts == 1 and sp % (A_PAIRS[-1][1] * Q_BLOCK) == 0 and nb % DEC_BLOCK == 0
    assert cache_a1_kv.shape[2] == A_PAIRS[0][0] and cache_a2_kv.shape[2] == A_PAIRS[1][0]
    assert cache_a3_kv.shape[2] == A_PAIRS[2][0]

    bias_prompt, bias_rev, bias_own = _bias_tables(a_rel_bias)
    w_in_b, w_out_b = w_in.astype(BF16), w_out.astype(BF16)
    w_cq_b, w_ckv_b, w_co_b = w_cq.astype(BF16), w_ckv.astype(BF16), w_co.astype(BF16)
    wa_b, wx_b = lru_wa.astype(BF16), lru_wx.astype(BF16)
    caches = (cache_a1_kv, cache_a2_kv, cache_a3_kv)
    mem_rows = mem_prompt.reshape(bp * N_MEM, D_MODEL)

    hp = x_prompt.reshape(bp * sp, D_MODEL)
    hs = x_sample.reshape(nb, D_MODEL)
    xn_p = rmsnorm_rows(hp, norm_mix[0], BF16, 512)
    a_p, a_s = [[], [], []], [[], [], []]
    ret_p, ret_s, conv_p, conv_s, lru_p, lru_s, memkv_p = [], [], [], [], [], [], []
    y_prompt = None
    for l in range(depth):
        lw = dict(conv_w=conv_w[l], conv_b=conv_b[l].reshape(1, C_WIDTH), wa=wa_b[l],
                  ba=lru_ba[l].reshape(1, C_WIDTH), wx=wx_b[l], bx=lru_bx[l].reshape(1, C_WIDTH),
                  lam=lru_lambda[l].reshape(1, C_WIDTH))
        last = l == depth - 1
        proj = matmul(xn_p, w_in_b[l])
        carry = None
        for g in range(A_GROUPS):
            carry = attn_group(proj, g, bias_prompt[g], carry, bp, sp)
        a_o = carry
        b_o, s1 = retention_prompt(proj, bp, sp)
        c_o, conv_new, h_last = conv_lru_prompt(proj, lw, bp, sp)
        mkv = matmul(rmsnorm_rows(mem_rows, norm_mem[l], BF16, 512), w_ckv_b[l])
        g_next = norm_final if last else norm_mix[l + 1]
        hp, nxt = fused_out(hp, a_o, b_o, c_o, w_out_b[l], norm_cross[l], w_cq_b[l],
                            mkv.reshape(bp, N_MEM, 2 * X_WIDTH), w_co_b[l], g_next,
                            F32 if last else BF16, bp, sp)
        if last:
            y_prompt = nxt.reshape(bp, sp, D_MODEL)
        else:
            xn_p = nxt
        p5 = proj.reshape(bp, sp, IN_WIDTH)
        for g, (win, _) in enumerate(A_PAIRS):
            n_keep = min(win, sp)
            kk = p5[:, sp - n_keep:, OFF_AK + g * A_WIDTH:OFF_AK + (g + 1) * A_WIDTH]
            vv = p5[:, sp - n_keep:, OFF_AV + g * A_WIDTH:OFF_AV + (g + 1) * A_WIDTH]
            a_p[g].append(jnp.stack([kk, vv], axis=2).reshape(bp, n_keep, 2, A_HEADS, A_HD))
        ret_p.append(s1)
        conv_p.append(conv_new)
        lru_p.append(h_last)
        memkv_p.append(mkv.reshape(bp, N_MEM, 2, X_HEADS, X_HD))
        proj_s = matmul(rmsnorm_rows(hs, norm_mix[l], BF16, nb), w_in_b[l])
        a_os = dec_self_attn(proj_s, caches, l, bias_rev, bias_own)
        b_os, s1_s = dec_retention(proj_s, state_ret, l)
        c_os, conv_new_s, h_s = dec_conv_lru(proj_s, state_conv, state_lru, lw, l)
        mixed = jnp.concatenate([a_os, b_os, c_os], axis=-1).astype(BF16)
        ys = matmul(mixed, w_out_b[l], res=hs)
        qs = matmul(rmsnorm_rows(ys, norm_cross[l], BF16, nb), w_cq_b[l])
        xo = dec_cross_attn(qs, cache_mem_kv, l).astype(BF16)
        hs = matmul(xo, w_co_b[l], res=ys)
        for g in range(A_GROUPS):
            kk = proj_s[:, OFF_AK + g * A_WIDTH:OFF_AK + (g + 1) * A_WIDTH]
            vv = proj_s[:, OFF_AV + g * A_WIDTH:OFF_AV + (g + 1) * A_WIDTH]
            a_s[g].append(jnp.stack([kk, vv], axis=1).reshape(nb, 1, 2, A_HEADS, A_HD))
        ret_s.append(s1_s)
        conv_s.append(conv_new_s)
        lru_s.append(h_s)
    y_sample = rmsnorm_rows(hs, norm_final, F32, nb).reshape(nb, ts, D_MODEL)
    a1_p, a2_p, a3_p = (jnp.stack(a_p[g]) for g in range(A_GROUPS))
    a1_s, a2_s, a3_s = (jnp.stack(a_s[g]) for g in range(A_GROUPS))
    return (y_prompt, y_sample, a1_p, a1_s, a2_p, a2_s, a3_p, a3_s,
            jnp.stack(ret_p), jnp.stack(ret_s), jnp.stack(conv_p), jnp.stack(conv_s),
            jnp.stack(lru_p), jnp.stack(lru_s), jnp.stack(memkv_p))
```
